```python
import math
import jax
import jax.numpy as jnp
from jax import lax
import numpy as np


D_MODEL = 1024
BATCH = 8
SEQ = 2048
DEPTH = 2

GRID_W = 64
CTX_LEN = 256
EPS = 1e-6
N_MOD = 6

RET_HEADS = 4
RET_DK = 128
RET_DV = 128
RET_CHUNK = 128
ROPE_BASE = 10000.0
FNET_GROUPS = 4
FNET_CH = 128
DN_HEADS = 4
DN_DK = 128
DN_DV = 128
DN_CHUNK = 64
DN_CONV = 3
SGU_GROUPS = 4
SGU_CH = 128
SGU_CHUNK = 128
MOE_GROUPS = 4
MOE_EXP_PER_GROUP = 8
MOE_EXPERTS = MOE_GROUPS * MOE_EXP_PER_GROUP
MOE_TOP_K = 2
MOE_HIDDEN = 512
MOE_BLOCK = 128

RET_QK = RET_HEADS * RET_DK
RET_V = RET_HEADS * RET_DV
FNET_W = FNET_GROUPS * FNET_CH
E_K = RET_QK
E_V = 2 * RET_QK
E_G = 2 * RET_QK + RET_V
E_F = 2 * RET_QK + 2 * RET_V
EVEN_IN = E_F + FNET_W
EVEN_MIX = RET_V + FNET_W
DN_QK = DN_HEADS * DN_DK
DN_V = DN_HEADS * DN_DV
SGU_W = SGU_GROUPS * SGU_CH
O_K = DN_QK
O_V = 2 * DN_QK
O_Z = 2 * DN_QK + DN_V
O_A = O_Z + DN_V
O_U = O_A + 4 * DN_HEADS
O_S = O_U + SGU_W
ODD_IN = O_S + SGU_W
ODD_MIX = DN_V + SGU_W

kernel_name = 'hybrid_prefix_diffusion_block'
F32 = jnp.float32


def rms_norm(x, w):
    xf = x.astype(F32)
    y = xf * lax.rsqrt(jnp.mean(xf * xf, -1, keepdims=True) + EPS)
    return (y * w.astype(F32)).astype(x.dtype)


def l2norm(x):
    return x * lax.rsqrt(jnp.sum(x * x, -1, keepdims=True) + EPS)


def to_heads(x, n_heads):
    b, t, _ = x.shape
    return x.reshape(b, t, n_heads, -1).transpose(0, 2, 1, 3).astype(F32)


def from_heads(x):
    b, h, t, d = x.shape
    return x.transpose(0, 2, 1, 3).reshape(b, t, h * d)


def to_chunks(x, c):
    return x.reshape(x.shape[:2] + (x.shape[2] // c, c) + x.shape[3:])


def from_chunks(x):
    return x.reshape(x.shape[:2] + (x.shape[2] * x.shape[3],) + x.shape[4:])


def _noflip(x):
    return x


def _flip(x):
    return x[:, :, ::-1]


def axial_rope(x):
    t, d = x.shape[2], x.shape[3]
    rows = t // GRID_W
    row = jnp.repeat(jnp.arange(rows), GRID_W).astype(F32)
    col = jnp.tile(jnp.arange(GRID_W), rows).astype(F32)
    n_freq = d // 4
    inv = jnp.power(ROPE_BASE, -jnp.arange(n_freq, dtype=F32) / n_freq)
    ang = jnp.concatenate([row[:, None] * inv, col[:, None] * inv], -1)
    cos, sin = jnp.cos(ang), jnp.sin(ang)
    x1, x2 = jnp.split(x, 2, -1)
    return jnp.concatenate([x1 * cos - x2 * sin, x1 * sin + x2 * cos], -1)


def _ret_states(k, v, log_gamma, s0):
    c = k.shape[3]
    pos = jnp.arange(c, dtype=F32)
    w_tail = jnp.exp(log_gamma[:, None] * (c - 1 - pos))
    kv = jnp.einsum('bhncd,bhnce->bhnde', k * w_tail[None, :, None, :, None], v)
    g_chunk = jnp.exp(log_gamma * c)[None, :, None, None]

    def step(s, kv_n):
        return g_chunk * s + kv_n, s

    s_fin, s_prev = lax.scan(step, s0, jnp.moveaxis(kv, 2, 0))
    return jnp.moveaxis(s_prev, 0, 2), s_fin


def _ret_out(q, k, v, log_gamma, s_prev):
    c = q.shape[3]
    pos = jnp.arange(c, dtype=F32)
    diff = pos[:, None] - pos[None, :]
    dmat = jnp.where(diff >= 0, jnp.exp(log_gamma[:, None, None] * jnp.maximum(diff, 0.0)), 0.0)
    scores = jnp.einsum('bhnid,bhnjd->bhnij', q, k) * dmat[None, :, None]
    inner = jnp.einsum('bhnij,bhnje->bhnie', scores, v)
    q_dec = q * jnp.exp(log_gamma[:, None] * (pos + 1.0))[None, :, None, :, None]
    return inner + jnp.einsum('bhnid,bhnde->bhnie', q_dec, s_prev)


def retention_dir(k, v, log_gamma, s0, q=None):
    kc, vc = to_chunks(k, RET_CHUNK), to_chunks(v, RET_CHUNK)
    s_prev, s_fin = _ret_states(kc, vc, log_gamma, s0)
    out = None if q is None else from_chunks(_ret_out(to_chunks(q, RET_CHUNK), kc, vc, log_gamma, s_prev))
    return out, s_fin


def group_norm_gate(o, gate, w):
    mu = jnp.mean(o, -1, keepdims=True)
    var = jnp.mean(jnp.square(o - mu), -1, keepdims=True)
    y = from_heads((o - mu) * lax.rsqrt(var + EPS)) * w.astype(F32)
    return (y * jax.nn.silu(gate.astype(F32))).astype(gate.dtype)


def fourier_mix(f):
    b, t, _ = f.shape
    z = jnp.fft.fft2(f.reshape(b, t, FNET_GROUPS, FNET_CH).astype(F32), axes=(1, 3), norm='ortho')
    return jnp.real(z).reshape(b, t, FNET_W).astype(f.dtype)


def short_conv(x, w):
    k = w.shape[0]
    y = lax.conv_general_dilated(x, w[:, None, :].astype(x.dtype), window_strides=(1,),
                                 padding=[((k - 1) // 2, k // 2)],
                                 dimension_numbers=('NWC', 'WIO', 'NWC'),
                                 feature_group_count=x.shape[-1])
    return jax.nn.silu(y)


def dn_gates(ab, d, a_log, dt_bias):
    b, t, _ = ab.shape
    a = ab[..., :2 * DN_HEADS].reshape(b, t, 2, DN_HEADS)[:, :, d].astype(F32)
    bt = ab[..., 2 * DN_HEADS:].reshape(b, t, 2, DN_HEADS)[:, :, d].astype(F32)
    beta = jax.nn.sigmoid(bt)
    glog = -jnp.exp(a_log[d].astype(F32)) * jax.nn.softplus(a + dt_bias[d].astype(F32))
    return beta.transpose(0, 2, 1), glog.transpose(0, 2, 1)


def _delta_prep(k, v, beta, glog):
    c = k.shape[3]
    g = jnp.cumsum(glog, -1)
    idx = jnp.arange(c)
    lower = idx[:, None] >= idx[None, :]
    strict = idx[:, None] > idx[None, :]
    diff = g[..., :, None] - g[..., None, :]
    decay = jnp.where(lower, jnp.exp(jnp.where(lower, diff, 0.0)), 0.0)
    kk = jnp.einsum('bhnid,bhnjd->bhnij', k, k)
    a_mat = jnp.where(strict, beta[..., :, None] * kk * decay, 0.0) + jnp.eye(c, dtype=F32)
    rhs = jnp.concatenate([v * beta[..., None], k * (beta * jnp.exp(g))[..., None]], -1)
    sol = lax.linalg.triangular_solve(a_mat, rhs, left_side=True, lower=True)
    dv = v.shape[-1]
    u, w = sol[..., :dv], sol[..., dv:]
    k_tail = k * jnp.exp(g[..., -1:] - g)[..., None]
    return u, w, k_tail, g, decay


def _delta_states(u, w, k_tail, g, s0):
    g_last = jnp.exp(g[..., -1])

    def step(s, xs):
        u_n, w_n, kt_n, gl_n = xs
        v_new = u_n - jnp.einsum('bhck,bhkv->bhcv', w_n, s)
        return s * gl_n[..., None, None] + jnp.einsum('bhck,bhcv->bhkv', kt_n, v_new), s

    xs = (jnp.moveaxis(u, 2, 0), jnp.moveaxis(w, 2, 0), jnp.moveaxis(k_tail, 2, 0), jnp.moveaxis(g_last, 2, 0))
    s_fin, s_prev = lax.scan(step, s0, xs)
    return jnp.moveaxis(s_prev, 0, 2), s_fin


def _delta_out(q, k, u, w, g, decay, s_prev):
    v_new = u - jnp.einsum('bhnck,bhnkv->bhncv', w, s_prev)
    attn = jnp.einsum('bhnik,bhnjk->bhnij', q, k) * decay
    inter = jnp.einsum('bhnik,bhnkv->bhniv', q * jnp.exp(g)[..., None], s_prev)
    return inter + jnp.einsum('bhnij,bhnjv->bhniv', attn, v_new)


def delta_dir(k, v, beta, glog, s0, q=None):
    kc = to_chunks(k, DN_CHUNK)
    u, w, k_tail, g, decay = _delta_prep(kc, to_chunks(v, DN_CHUNK), to_chunks(beta, DN_CHUNK), to_chunks(glog, DN_CHUNK))
    s_prev, s_fin = _delta_states(u, w, k_tail, g, s0)
    out = None if q is None else from_chunks(_delta_out(to_chunks(q, DN_CHUNK), kc, u, w, g, decay, s_prev))
    return out, s_fin


def gated_head_norm(o, z, w):
    y = o * lax.rsqrt(jnp.mean(o * o, -1, keepdims=True) + EPS) * w.astype(F32)
    return (from_heads(y) * jax.nn.silu(z.astype(F32))).astype(z.dtype)


def spatial_gating(u, v, w_s, b_s):
    b, t, _ = u.shape
    u = jax.nn.gelu(u, approximate=False)
    v = jax.nn.gelu(v.astype(F32), approximate=False).reshape(b, t // SGU_CHUNK, SGU_CHUNK, SGU_GROUPS, SGU_CH)
    mu = jnp.mean(v, -1, keepdims=True)
    var = jnp.mean(jnp.square(v - mu), -1, keepdims=True)
    v = (v - mu) * lax.rsqrt(var + EPS)
    v = jnp.einsum('gpq,bnqgc->bnpgc', w_s.astype(F32), v) + b_s.astype(F32).T[None, None, :, :, None]
    return (u.astype(F32) * v.reshape(b, t, SGU_W)).astype(u.dtype)


def even_mixer(h_ctx, h_lat, w_in, decay_logit, gn_w, w_out, need_ctx):
    log_gamma = jax.nn.log_sigmoid(decay_logit.astype(F32))
    scale = RET_DK ** -0.5
    lat = h_lat @ w_in
    q_l = axial_rope(to_heads(lat[..., :E_K], RET_HEADS)) * scale
    k_l = axial_rope(to_heads(lat[..., E_K:E_V], RET_HEADS))
    v_l = to_heads(lat[..., E_V:E_G], RET_HEADS)
    if need_ctx:
        ctx = h_ctx @ w_in
        q_c = to_heads(ctx[..., :E_K], RET_HEADS) * scale
        kv_c = ctx[..., E_K:E_G]
    else:
        kv_c = h_ctx @ w_in[:, E_K:E_G]
    k_c = to_heads(kv_c[..., :RET_QK], RET_HEADS)
    v_c = to_heads(kv_c[..., RET_QK:], RET_HEADS)
    s_zero = jnp.zeros((h_lat.shape[0], RET_HEADS, RET_DK, RET_DV), F32)
    o_l, o_c = 0.0, 0.0
    for d, fl in enumerate((_noflip, _flip)):
        oc, s_c = retention_dir(fl(k_c), fl(v_c), log_gamma[d], s_zero, fl(q_c) if need_ctx else None)
        ol, _ = retention_dir(fl(k_l), fl(v_l), log_gamma[d], s_c, fl(q_l))
        o_l = o_l + fl(ol)
        if need_ctx:
            o_c = o_c + fl(oc)
    y_l = jnp.concatenate([group_norm_gate(o_l, lat[..., E_G:E_F], gn_w), fourier_mix(lat[..., E_F:])], -1) @ w_out
    y_c = None
    if need_ctx:
        y_c = jnp.concatenate([group_norm_gate(o_c, ctx[..., E_G:E_F], gn_w), fourier_mix(ctx[..., E_F:])], -1) @ w_out
    return y_c, y_l


def odd_mixer(h_ctx, h_lat, w_in, conv_w, a_log, dt_bias, norm_w, sp_w, sp_b, w_out, need_ctx):
    scale = DN_DK ** -0.5
    lat = h_lat @ w_in
    qkv_l = short_conv(lat[..., :O_Z], conv_w)
    q_l = l2norm(to_heads(qkv_l[..., :O_K], DN_HEADS)) * scale
    k_l = l2norm(to_heads(qkv_l[..., O_K:O_V], DN_HEADS))
    v_l = to_heads(qkv_l[..., O_V:], DN_HEADS)
    ab_l = lat[..., O_A:O_U]
    if need_ctx:
        ctx = h_ctx @ w_in
        qkv_c = short_conv(ctx[..., :O_Z], conv_w)
        q_c = l2norm(to_heads(qkv_c[..., :O_K], DN_HEADS)) * scale
        kv_c = qkv_c[..., O_K:]
        ab_c = ctx[..., O_A:O_U]
    else:
        kv_c = short_conv(h_ctx @ w_in[:, O_K:O_Z], conv_w[:, O_K:O_Z])
        ab_c = h_ctx @ w_in[:, O_A:O_U]
    k_c = l2norm(to_heads(kv_c[..., :DN_QK], DN_HEADS))
    v_c = to_heads(kv_c[..., DN_QK:], DN_HEADS)
    s_zero = jnp.zeros((h_lat.shape[0], DN_HEADS, DN_DK, DN_DV), F32)
    o_l, o_c = 0.0, 0.0
    for d, fl in enumerate((_noflip, _flip)):
        beta_c, glog_c = dn_gates(ab_c, d, a_log, dt_bias)
        beta_l, glog_l = dn_gates(ab_l, d, a_log, dt_bias)
        oc, s_c = delta_dir(fl(k_c), fl(v_c), fl(beta_c), fl(glog_c), s_zero, fl(q_c) if need_ctx else None)
        ol, _ = delta_dir(fl(k_l), fl(v_l), fl(beta_l), fl(glog_l), s_c, fl(q_l))
        o_l = o_l + fl(ol)
        if need_ctx:
            o_c = o_c + fl(oc)
    y_l = jnp.concatenate([gated_head_norm(o_l, lat[..., O_Z:O_A], norm_w),
                           spatial_gating(lat[..., O_U:O_S], lat[..., O_S:], sp_w, sp_b)], -1) @ w_out
    y_c = None
    if need_ctx:
        y_c = jnp.concatenate([gated_head_norm(o_c, ctx[..., O_Z:O_A], norm_w),
                               spatial_gating(ctx[..., O_U:O_S], ctx[..., O_S:], sp_w, sp_b)], -1) @ w_out
    return y_c, y_l


def hier_moe(h, wg, bg, we, be, w_gate, w_up, w_down):
    t, d = h.shape
    logit_g = (h @ wg).astype(F32) + bg.astype(F32)
    grp = jnp.argmax(logit_g, -1)
    p_grp = jnp.take_along_axis(jax.nn.softmax(logit_g, -1), grp[:, None], -1)[:, 0]
    logit_e = ((h @ we).astype(F32) + be.astype(F32)).reshape(t, MOE_GROUPS, MOE_EXP_PER_GROUP)
    logit_in = jnp.take_along_axis(logit_e, grp[:, None, None], 1)[:, 0]
    top_l, top_i = lax.top_k(logit_in, MOE_TOP_K)
    gate = p_grp[:, None] * jax.nn.softmax(top_l, -1)
    expert = grp[:, None] * MOE_EXP_PER_GROUP + top_i
    n_assign = t * MOE_TOP_K
    e_flat = expert.reshape(-1)
    g_flat = gate.reshape(-1)
    tok = jnp.repeat(jnp.arange(t), MOE_TOP_K)
    order = jnp.argsort(e_flat)
    e_s, tok_s, g_s = e_flat[order], tok[order], g_flat[order]
    counts = jnp.bincount(e_flat, length=MOE_EXPERTS)
    starts = jnp.cumsum(counts) - counts
    padded = (counts + MOE_BLOCK - 1) // MOE_BLOCK * MOE_BLOCK
    p_ends = jnp.cumsum(padded)
    dest = p_ends[e_s] - padded[e_s] + jnp.arange(n_assign) - starts[e_s]
    n_blocks = -(-(n_assign + MOE_EXPERTS * (MOE_BLOCK - 1)) // MOE_BLOCK)
    rows = jnp.zeros((n_blocks * MOE_BLOCK, d), h.dtype).at[dest].set(h[tok_s])
    blk_exp = jnp.minimum(jnp.searchsorted(p_ends, jnp.arange(n_blocks) * MOE_BLOCK, side='right'), MOE_EXPERTS - 1)

    def run_block(args):
        xb, e = args
        return (jax.nn.silu(xb @ w_gate[e]) * (xb @ w_up[e])) @ w_down[e]

    y = lax.map(run_block, (rows.reshape(n_blocks, MOE_BLOCK, d), blk_exp)).reshape(-1, d)
    contrib = y[dest].astype(F32) * g_s[:, None]
    return jnp.zeros((t, d), F32).at[tok_s].add(contrib).astype(h.dtype)


def setup_inputs(seed: int = 0) -> dict:
    key = jax.random.key(seed)
    ks = iter(jax.random.split(key, 40))
    n_even = (DEPTH + 1) // 2
    n_odd = DEPTH // 2

    def nrm(shape, scale=1.0):
        return jax.random.normal(next(ks), shape, F32) * scale

    def gain(shape):
        return 1.0 + nrm(shape, 0.02)

    x = nrm((BATCH, SEQ, D_MODEL))
    c = nrm((BATCH, D_MODEL))
    ctx = nrm((BATCH, CTX_LEN, D_MODEL))
    c_ctx = nrm((D_MODEL,))
    norm1_w = gain((DEPTH, D_MODEL))
    norm2_w = gain((DEPTH, D_MODEL))
    ada_w = nrm((DEPTH, D_MODEL, N_MOD * D_MODEL), 0.5 * D_MODEL ** -0.5)
    ada_b = nrm((DEPTH, N_MOD * D_MODEL), 0.01)
    even_w_in = nrm((n_even, D_MODEL, EVEN_IN), D_MODEL ** -0.5)
    ret_base = jnp.log(jnp.power(2.0, 5.0 + jnp.arange(RET_HEADS, dtype=F32)) - 1.0)
    ret_decay_logit = ret_base + nrm((n_even, 2, RET_HEADS), 0.1)
    ret_gn_w = gain((n_even, RET_V))
    even_w_out = nrm((n_even, EVEN_MIX, D_MODEL), EVEN_MIX ** -0.5)
    odd_w_in = nrm((n_odd, D_MODEL, ODD_IN), D_MODEL ** -0.5)
    dn_conv_w = nrm((n_odd, DN_CONV, O_Z), DN_CONV ** -0.5)
    dn_a_log = jnp.log(jax.random.uniform(next(ks), (n_odd, 2, DN_HEADS), F32, 1.0, 16.0))
    dt = jnp.exp(jax.random.uniform(next(ks), (n_odd, 2, DN_HEADS), F32, math.log(1e-3), math.log(1e-1)))
    dn_dt_bias = dt + jnp.log(-jnp.expm1(-dt))
    dn_norm_w = gain((n_odd, DN_DV))
    sgu_w = nrm((n_odd, SGU_GROUPS, SGU_CHUNK, SGU_CHUNK), SGU_CHUNK ** -0.5)
    sgu_b = gain((n_odd, SGU_GROUPS, SGU_CHUNK))
    odd_w_out = nrm((n_odd, ODD_MIX, D_MODEL), ODD_MIX ** -0.5)
    router_g_w = nrm((DEPTH, D_MODEL, MOE_GROUPS), D_MODEL ** -0.5)
    router_g_b = nrm((DEPTH, MOE_GROUPS), 0.01)
    router_e_w = nrm((DEPTH, D_MODEL, MOE_EXPERTS), D_MODEL ** -0.5)
    router_e_b = nrm((DEPTH, MOE_EXPERTS), 0.01)
    moe_w_gate = nrm((DEPTH, MOE_EXPERTS, D_MODEL, MOE_HIDDEN), D_MODEL ** -0.5)
    moe_w_up = nrm((DEPTH, MOE_EXPERTS, D_MODEL, MOE_HIDDEN), D_MODEL ** -0.5)
    moe_w_down = nrm((DEPTH, MOE_EXPERTS, MOE_HIDDEN, D_MODEL), MOE_HIDDEN ** -0.5)
    final_norm_w = gain((D_MODEL,))
    return {'x': x, 'c': c, 'ctx': ctx, 'c_ctx': c_ctx, 'norm1_w': norm1_w, 'norm2_w': norm2_w,
            'ada_w': ada_w, 'ada_b': ada_b, 'even_w_in': even_w_in, 'ret_decay_logit': ret_decay_logit,
            'ret_gn_w': ret_gn_w, 'even_w_out': even_w_out, 'odd_w_in': odd_w_in, 'dn_conv_w': dn_conv_w,
            'dn_a_log': dn_a_log, 'dn_dt_bias': dn_dt_bias, 'dn_norm_w': dn_norm_w, 'sgu_w': sgu_w,
            'sgu_b': sgu_b, 'odd_w_out': odd_w_out, 'router_g_w': router_g_w, 'router_g_b': router_g_b,
            'router_e_w': router_e_w, 'router_e_b': router_e_b, 'moe_w_gate': moe_w_gate,
            'moe_w_up': moe_w_up, 'moe_w_down': moe_w_down, 'final_norm_w': final_norm_w}


def reference(x, c, ctx, c_ctx, norm1_w, norm2_w, ada_w, ada_b, even_w_in, ret_decay_logit, ret_gn_w,
              even_w_out, odd_w_in, dn_conv_w, dn_a_log, dn_dt_bias, dn_norm_w, sgu_w, sgu_b, odd_w_out,
              router_g_w, router_g_b, router_e_w, router_e_b, moe_w_gate, moe_w_up, moe_w_down, final_norm_w):
    b, s, d = x.shape
    x_lat, x_ctx = x, ctx
    for layer in range(DEPTH):
        last = layer == DEPTH - 1
        mod_l = (jax.nn.silu(c) @ ada_w[layer] + ada_b[layer])[:, None, :]
        mod_c = jax.nn.silu(c_ctx) @ ada_w[layer] + ada_b[layer]
        sh1_l, sc1_l, g1_l, sh2_l, sc2_l, g2_l = jnp.split(mod_l, N_MOD, -1)
        sh1_c, sc1_c, g1_c, sh2_c, sc2_c, g2_c = jnp.split(mod_c, N_MOD, -1)
        h_l = rms_norm(x_lat, norm1_w[layer]) * (1.0 + sc1_l) + sh1_l
        h_c = rms_norm(x_ctx, norm1_w[layer]) * (1.0 + sc1_c) + sh1_c
        if layer % 2 == 0:
            i = layer // 2
            y_c, y_l = even_mixer(h_c, h_l, even_w_in[i], ret_decay_logit[i], ret_gn_w[i], even_w_out[i], not last)
        else:
            i = layer // 2
            y_c, y_l = odd_mixer(h_c, h_l, odd_w_in[i], dn_conv_w[i], dn_a_log[i], dn_dt_bias[i], dn_norm_w[i],
                                 sgu_w[i], sgu_b[i], odd_w_out[i], not last)
        x_lat = x_lat + g1_l * y_l
        h2_l = rms_norm(x_lat, norm2_w[layer]) * (1.0 + sc2_l) + sh2_l
        moe_args = (router_g_w[layer], router_g_b[layer], router_e_w[layer], router_e_b[layer],
                    moe_w_gate[layer], moe_w_up[layer], moe_w_down[layer])
        if last:
            x_lat = x_lat + g2_l * hier_moe(h2_l.reshape(-1, d), *moe_args).reshape(b, s, d)
        else:
            x_ctx = x_ctx + g1_c * y_c
            h2_c = rms_norm(x_ctx, norm2_w[layer]) * (1.0 + sc2_c) + sh2_c
            n_c = h2_c.shape[0] * h2_c.shape[1]
            out = hier_moe(jnp.concatenate([h2_c.reshape(-1, d), h2_l.reshape(-1, d)], 0), *moe_args)
            x_ctx = x_ctx + g2_c * out[:n_c].reshape(x_ctx.shape)
            x_lat = x_lat + g2_l * out[n_c:].reshape(b, s, d)
    return rms_norm(x_lat, final_norm_w)
```

```python
import functools
import math

import jax
import jax.numpy as jnp
from jax import lax
from jax.experimental import pallas as pl
from jax.experimental.pallas import tpu as pltpu

F32 = jnp.float32
BF16 = jnp.bfloat16
I32 = jnp.int32
HIGHEST = lax.Precision.HIGHEST

D_MODEL = 1024
BATCH = 8
SEQ = 2048
DEPTH = 2
GRID_W = 64
CTX_LEN = 256
EPS = 1e-6
N_MOD = 6
HEADS = 4
HEAD_DIM = 128
RET_CHUNK = 128
ROPE_BASE = 10000.0
FNET_GROUPS = 4
FNET_CH = 128
DN_CHUNK = 64
SGU_GROUPS = 4
SGU_CH = 128
SGU_CHUNK = 128
MOE_GROUPS = 4
MOE_EXP_PER_GROUP = 8
MOE_EXPERTS = 32
MOE_HIDDEN = 512
MOE_BLOCK = 128
LANES = 128
MOD_ROWS = 16
CTX_MOD_ROW = BATCH

N_LAT = BATCH * SEQ
N_CTX = BATCH * CTX_LEN


def _cparams(sem, vmem_mb=48):
    return pltpu.CompilerParams(dimension_semantics=sem, vmem_limit_bytes=vmem_mb * 1024 * 1024)


def _dot(a, b, prec=None):
    return jnp.dot(a, b, preferred_element_type=F32, precision=prec)


def _dot_nt(a, b):
    return lax.dot_general(a, b, (((1,), (1,)), ((), ())), preferred_element_type=F32)


def _dot_tn(a, b):
    return lax.dot_general(a, b, (((0,), (0,)), ((), ())), preferred_element_type=F32)


def _silu(x):
    return x * jax.nn.sigmoid(x)


def _gelu(x):
    return 0.5 * x * (1.0 + lax.erf(x * (2.0 ** -0.5)))


def _mod_kernel(c_ref, w_ref, b_ref, o_ref):
    s = _silu(c_ref[...])
    o_ref[...] = _dot(s, w_ref[...], HIGHEST) + b_ref[...]


def _modulation(c, c_ctx, ada_w, ada_b):
    tn = 768
    n = N_MOD * D_MODEL
    cvec = jnp.concatenate([c, c_ctx[None, :], jnp.zeros((MOD_ROWS - BATCH - 1, D_MODEL), F32)], 0)
    out = pl.pallas_call(
        _mod_kernel,
        grid=(DEPTH, n // tn),
        in_specs=[
            pl.BlockSpec((MOD_ROWS, D_MODEL), lambda l, j: (0, 0)),
            pl.BlockSpec((None, D_MODEL, tn), lambda l, j: (l, 0, j)),
            pl.BlockSpec((None, 1, tn), lambda l, j: (l, 0, j)),
        ],
        out_specs=pl.BlockSpec((None, MOD_ROWS, tn), lambda l, j: (l, 0, j)),
        out_shape=jax.ShapeDtypeStruct((DEPTH, MOD_ROWS, n), F32),
        compiler_params=_cparams(("arbitrary", "arbitrary")),
        name="adaln_mod",
    )(cvec, ada_w, ada_b.reshape(DEPTH, 1, n))
    return out.reshape(DEPTH, MOD_ROWS, N_MOD, D_MODEL)


def _inproj_kernel(*refs, segs, n_out, rope, q_scale):
    x_ref, nw_ref, mod_ref, w_ref = refs[:4]
    pos = 4
    if rope:
        cs_ref, sn_ref = refs[4:6]
        pos = 6
    out_refs = refs[pos:pos + n_out]
    h_scr = refs[pos + n_out]
    x = x_ref[...]
    h = x * lax.rsqrt(jnp.mean(x * x, -1, keepdims=True) + EPS) * nw_ref[...]
    h = h * (1.0 + mod_ref[1:2, :]) + mod_ref[0:1, :]
    h_scr[...] = h.astype(BF16)
    for c0, width, kind, oi, o0 in segs:
        acc = _dot(h_scr[...], w_ref[:, c0:c0 + width])
        if kind in ("q", "k"):
            for j in range(width // HEAD_DIM):
                a = acc[:, j * HEAD_DIM:(j + 1) * HEAD_DIM]
                if rope:
                    a = a * cs_ref[...] + pltpu.roll(a, HEAD_DIM // 2, 1) * sn_ref[...]
                if kind == "q":
                    a = a * q_scale
                out_refs[oi][:, o0 + j * HEAD_DIM:o0 + (j + 1) * HEAD_DIM] = a.astype(out_refs[oi].dtype)
        else:
            out_refs[oi][:, o0:o0 + width] = acc.astype(out_refs[oi].dtype)


def _inproj(x, norm_w, mod, w, segs, outs, *, tm, rows_per_mod, rope_tabs=None, q_scale=1.0):
    rows = x.shape[0]
    n_in = w.shape[1]
    if rows_per_mod is None:
        mod_map = lambda i: (CTX_MOD_ROW, 0, 0)
    else:
        tiles_per_mod = rows_per_mod // tm
        mod_map = lambda i: (i // tiles_per_mod, 0, 0)
    in_specs = [
        pl.BlockSpec((tm, D_MODEL), lambda i: (i, 0)),
        pl.BlockSpec((1, D_MODEL), lambda i: (0, 0)),
        pl.BlockSpec((None, N_MOD, D_MODEL), mod_map),
        pl.BlockSpec((D_MODEL, n_in), lambda i: (0, 0)),
    ]
    args = [x, norm_w.reshape(1, D_MODEL), mod, w]
    if rope_tabs is not None:
        tiles_per_seq = SEQ // tm
        for t in rope_tabs:
            in_specs.append(pl.BlockSpec((tm, HEAD_DIM), lambda i: (i % tiles_per_seq, 0)))
            args.append(t)
    out_specs = [pl.BlockSpec((tm, wd), lambda i: (i, 0)) for wd, _ in outs]
    out_shape = [jax.ShapeDtypeStruct((rows, wd), dt) for wd, dt in outs]
    return pl.pallas_call(
        functools.partial(_inproj_kernel, segs=segs, n_out=len(outs), rope=rope_tabs is not None, q_scale=q_scale),
        grid=(rows // tm,),
        in_specs=in_specs,
        out_specs=out_specs,
        out_shape=out_shape,
        scratch_shapes=[pltpu.VMEM((tm, D_MODEL), BF16)],
        compiler_params=_cparams(("arbitrary",)),
        name="inproj",
    )(*args)


def _rope_tables():
    rows = SEQ // GRID_W
    row = jnp.repeat(jnp.arange(rows), GRID_W).astype(F32)
    col = jnp.tile(jnp.arange(GRID_W), rows).astype(F32)
    n_freq = HEAD_DIM // 4
    inv = jnp.power(ROPE_BASE, -jnp.arange(n_freq, dtype=F32) / n_freq)
    ang = jnp.concatenate([row[:, None] * inv, col[:, None] * inv], -1)
    cos, sin = jnp.cos(ang), jnp.sin(ang)
    return jnp.concatenate([cos, cos], -1), jnp.concatenate([-sin, sin], -1)


def _ret_kernel(ql_ref, kl_ref, vl_ref, gl_ref, qc_ref, kc_ref, vc_ref, gc_ref, dl_ref, gnw_ref,
                yl_ref, yc_ref, s_scr, ol_scr, oc_scr, dm_scr, qd_scr, wt_scr):
    c = RET_CHUNK
    x = dl_ref[...]
    lg = jnp.minimum(x, 0.0) - jnp.log(1.0 + jnp.exp(-jnp.abs(x)))
    ii = lax.broadcasted_iota(I32, (c, c), 0).astype(F32)
    jj = lax.broadcasted_iota(I32, (c, c), 1).astype(F32)
    pp = lax.broadcasted_iota(I32, (c, HEAD_DIM), 0).astype(F32)
    gch = []
    for d in range(2):
        lgd = lg[d:d + 1, :]
        diff = (ii - jj) if d == 0 else (jj - ii)
        dm_scr[d] = jnp.where(diff >= 0, jnp.exp(lgd * jnp.maximum(diff, 0.0)), 0.0)
        pos = pp if d == 0 else (c - 1.0 - pp)
        qd_scr[d] = jnp.exp(lgd * (pos + 1.0))
        wt_scr[d] = jnp.exp(lgd * (c - 1.0 - pos))
        gch.append(jnp.exp(lgd * c))
    s_scr[...] = jnp.zeros_like(s_scr)
    ol_scr[...] = jnp.zeros_like(ol_scr)
    oc_scr[...] = jnp.zeros_like(oc_scr)

    def step(d, q_ref, k_ref, v_ref, o_scr, n):
        r = pl.ds(pl.multiple_of(n * c, c), c)
        q, k, v = q_ref[r, :], k_ref[r, :], v_ref[r, :]
        s = s_scr[d]
        sb = s.astype(BF16)
        pm = (_dot_nt(q, k) * dm_scr[d]).astype(BF16)
        qd = (q.astype(F32) * qd_scr[d]).astype(BF16)
        o_scr[r, :] += _dot(pm, v) + _dot(qd, sb)
        kw = (k.astype(F32) * wt_scr[d]).astype(BF16)
        s_scr[d] = gch[d] * s + _dot_tn(kw, v)

    n_c = CTX_LEN // c
    n_l = SEQ // c
    for n in range(n_c):
        step(0, qc_ref, kc_ref, vc_ref, oc_scr, n)
        step(1, qc_ref, kc_ref, vc_ref, oc_scr, n_c - 1 - n)

    def body(n, carry):
        step(0, ql_ref, kl_ref, vl_ref, ol_scr, n)
        step(1, ql_ref, kl_ref, vl_ref, ol_scr, n_l - 1 - n)
        return carry

    lax.fori_loop(0, n_l, body, 0)

    def finish(o_scr, g_ref, y_ref, rows):
        rb = 256

        def fbody(t, carry):
            r = pl.ds(pl.multiple_of(t * rb, rb), rb)
            o = o_scr[r, :]
            mu = jnp.mean(o, -1, keepdims=True)
            oc = o - mu
            var = jnp.mean(oc * oc, -1, keepdims=True)
            y = oc * lax.rsqrt(var + EPS) * gnw_ref[...]
            y_ref[r, :] = (y * _silu(g_ref[r, :].astype(F32))).astype(y_ref.dtype)
            return carry

        lax.fori_loop(0, rows // rb, fbody, 0)

    finish(oc_scr, gc_ref, yc_ref, CTX_LEN)
    finish(ol_scr, gl_ref, yl_ref, SEQ)


def _retention(lat_l, lat_c, decay_logit, gn_w):
    dl = decay_logit.T.reshape(HEADS, 2, 1)
    gnw = gn_w.reshape(HEADS, 1, HEAD_DIM)

    def cols(rows, off):
        return pl.BlockSpec((rows, HEAD_DIM), lambda b, h: (b, off + h))

    in_specs = ([cols(SEQ, j * HEADS) for j in range(4)] + [cols(CTX_LEN, j * HEADS) for j in range(4)]
                + [pl.BlockSpec((None, 2, 1), lambda b, h: (h, 0, 0)),
                   pl.BlockSpec((None, 1, HEAD_DIM), lambda b, h: (h, 0, 0))])
    return pl.pallas_call(
        _ret_kernel,
        grid=(BATCH, HEADS),
        in_specs=in_specs,
        out_specs=[cols(SEQ, 0), cols(CTX_LEN, 0)],
        out_shape=[jax.ShapeDtypeStruct((N_LAT, HEADS * HEAD_DIM), BF16),
                   jax.ShapeDtypeStruct((N_CTX, HEADS * HEAD_DIM), BF16)],
        scratch_shapes=[
            pltpu.VMEM((2, HEAD_DIM, HEAD_DIM), F32),
            pltpu.VMEM((SEQ, HEAD_DIM), F32),
            pltpu.VMEM((CTX_LEN, HEAD_DIM), F32),
            pltpu.VMEM((2, RET_CHUNK, RET_CHUNK), F32),
            pltpu.VMEM((2, RET_CHUNK, HEAD_DIM), F32),
            pltpu.VMEM((2, RET_CHUNK, HEAD_DIM), F32),
        ],
        compiler_params=_cparams(("arbitrary", "arbitrary")),
        name="retention",
    )(lat_l, lat_l, lat_l, lat_l, lat_c, lat_c, lat_c, lat_c, dl, gnw)


def _fourier_kernel(f_ref, cc_ref, w_ref, o_ref, x_scr, *, t):
    @pl.when(pl.program_id(1) == 0)
    def _():
        for g in range(FNET_GROUPS):
            sl = slice(g * FNET_CH, (g + 1) * FNET_CH)
            r = _dot(f_ref[:, sl], cc_ref[...])
            x_scr[0:t, sl] = r[:, :FNET_CH].astype(BF16)
            x_scr[t:2 * t, sl] = r[:, FNET_CH:].astype(BF16)

    o_ref[...] = _dot(w_ref[...], x_scr[...]).astype(o_ref.dtype)


def _dft_tables(t):
    lo_n = 32
    hi_n = t // lo_n
    kk = jnp.arange(t, dtype=I32)[:, None]

    def cs(idx):
        ang = ((kk * idx[None, :]) % t).astype(F32) * (2.0 * math.pi / t)
        return jnp.cos(ang), jnp.sin(ang)

    c_hi, s_hi = cs(jnp.arange(hi_n, dtype=I32) * lo_n)
    c_lo, s_lo = cs(jnp.arange(lo_n, dtype=I32))
    scale = t ** -0.5
    ct = (c_hi[:, :, None] * c_lo[:, None, :] - s_hi[:, :, None] * s_lo[:, None, :]).reshape(t, t)
    st = (s_hi[:, :, None] * c_lo[:, None, :] + c_hi[:, :, None] * s_lo[:, None, :]).reshape(t, t)
    return jnp.concatenate([ct * scale, st * (-scale)], 1).astype(BF16)


def _fourier(lat, t, tmf):
    fw = FNET_GROUPS * FNET_CH
    ch = jnp.arange(FNET_CH, dtype=I32)
    ang = ((ch[:, None] * ch[None, :]) % FNET_CH).astype(F32) * (2.0 * math.pi / FNET_CH)
    cc = (jnp.concatenate([jnp.cos(ang), jnp.sin(ang)], 1) * (FNET_CH ** -0.5)).astype(BF16)
    w = _dft_tables(t)
    n_m = t // tmf
    fcol = lat.shape[1] // fw - 1
    return pl.pallas_call(
        functools.partial(_fourier_kernel, t=t),
        grid=(BATCH, n_m),
        in_specs=[
            pl.BlockSpec((t, fw), lambda b, m: (b, fcol)),
            pl.BlockSpec((FNET_CH, 2 * FNET_CH), lambda b, m: (0, 0)),
            pl.BlockSpec((tmf, 2 * t), lambda b, m: (m, 0)),
        ],
        out_specs=pl.BlockSpec((tmf, fw), lambda b, m: (b * n_m + m, 0)),
        out_shape=jax.ShapeDtypeStruct((BATCH * t, fw), BF16),
        scratch_shapes=[pltpu.VMEM((2 * t, fw), BF16)],
        compiler_params=_cparams(("arbitrary", "arbitrary")),
        name="fourier",
    )(lat, cc, w)


def _outproj_kernel(*refs, tm, n_ctx_tiles):
    n_str = 2 if n_ctx_tiles else 1
    streams = [refs[3 * j:3 * j + 3] for j in range(n_str)]
    w_ref, mod_ref, nw_ref, wr_ref, br_ref = refs[3 * n_str:3 * n_str + 5]
    xo_refs = refs[3 * n_str + 5:4 * n_str + 5]
    h2_ref, ri_ref, rg_ref, cnt_ref, carry_scr = refs[4 * n_str + 5:]
    i = pl.program_id(0)

    @pl.when(i == 0)
    def _():
        carry_scr[...] = jnp.zeros_like(carry_scr)

    args = (w_ref, mod_ref, nw_ref, wr_ref, br_ref, h2_ref, ri_ref, rg_ref, cnt_ref, carry_scr)
    if n_ctx_tiles:
        @pl.when(i < n_ctx_tiles)
        def _():
            _outproj_tile(*streams[0], xo_refs[0], *args, tm=tm)

        @pl.when(i >= n_ctx_tiles)
        def _():
            _outproj_tile(*streams[1], xo_refs[1], *args, tm=tm)
    else:
        _outproj_tile(*streams[0], xo_refs[0], *args, tm=tm)


def _outproj_tile(ya_ref, yb_ref, x_ref, xo_ref, w_ref, mod_ref, nw_ref, wr_ref, br_ref,
                  h2_ref, ri_ref, rg_ref, cnt_ref, carry_scr, *, tm):
    half = ya_ref.shape[1]
    y = _dot(ya_ref[...], w_ref[0:half, :]) + _dot(yb_ref[...], w_ref[half:2 * half, :])
    xn = x_ref[...] + mod_ref[2:3, :] * y
    xo_ref[...] = xn
    h2 = xn * lax.rsqrt(jnp.mean(xn * xn, -1, keepdims=True) + EPS) * nw_ref[...]
    h2 = h2 * (1.0 + mod_ref[4:5, :]) + mod_ref[3:4, :]
    h2_ref[...] = h2
    logit = _dot(h2, wr_ref[...], HIGHEST) + br_ref[...]
    lane = lax.broadcasted_iota(I32, (tm, LANES), 1)
    big = jnp.int32(1 << 20)
    neg = jnp.float32(-jnp.inf)
    lg = jnp.where(lane < MOE_GROUPS, logit, neg)
    mg = jnp.max(lg, -1, keepdims=True)
    grp = jnp.min(jnp.where(lg == mg, lane, big), -1, keepdims=True)
    p_grp = 1.0 / jnp.sum(jnp.exp(lg - mg), -1, keepdims=True)
    lo = MOE_GROUPS + grp * MOE_EXP_PER_GROUP
    le = jnp.where((lane >= lo) & (lane < lo + MOE_EXP_PER_GROUP), logit, neg)
    m1 = jnp.max(le, -1, keepdims=True)
    i1 = jnp.min(jnp.where(le == m1, lane, big), -1, keepdims=True)
    le2 = jnp.where(lane == i1, neg, le)
    m2 = jnp.max(le2, -1, keepdims=True)
    i2 = jnp.min(jnp.where(le2 == m2, lane, big), -1, keepdims=True)
    t2 = jnp.exp(m2 - m1)
    w1 = p_grp / (1.0 + t2)
    w2 = w1 * t2
    e1 = i1 - MOE_GROUPS
    e2 = i2 - MOE_GROUPS
    oh1 = jnp.where(lane == e1, 1.0, 0.0)
    oh2 = jnp.where(lane == e2, 1.0, 0.0)
    ri = lax.broadcasted_iota(I32, (tm, tm), 0)
    ci = lax.broadcasted_iota(I32, (tm, tm), 1)
    lt = jnp.where(ri > ci, 1.0, 0.0).astype(BF16)
    carry = carry_scr[...]
    cnt1 = jnp.sum(oh1, 0, keepdims=True)
    pre1 = _dot(lt, oh1.astype(BF16)) + carry
    pre2 = _dot(lt, oh2.astype(BF16)) + carry + cnt1
    rk1 = jnp.sum(oh1 * pre1, -1, keepdims=True).astype(I32)
    rk2 = jnp.sum(oh2 * pre2, -1, keepdims=True).astype(I32)
    new_carry = carry + cnt1 + jnp.sum(oh2, 0, keepdims=True)
    carry_scr[...] = new_carry
    cnt_ref[...] = new_carry
    ri_ref[...] = jnp.where(lane == 0, e1, jnp.where(lane == 1, e2, jnp.where(lane == 2, rk1, jnp.where(lane == 3, rk2, 0))))
    rg_ref[...] = jnp.where(lane == 0, w1, jnp.where(lane == 1, w2, 0.0))


def _outproj(streams, w_out, mod, norm_w, wr, br, *, tm):
    n_ctx_tiles = streams[0][2].shape[0] // tm if len(streams) == 2 else 0
    lat_tiles = streams[-1][2].shape[0] // tm
    half = streams[0][0].shape[1]
    tiles_per_mod = SEQ // tm
    total = n_ctx_tiles + lat_tiles

    def tile_maps(is_ctx):
        if is_ctx:
            return lambda i: (jnp.minimum(i, n_ctx_tiles - 1), 0)
        return lambda i: (jnp.maximum(i - n_ctx_tiles, 0), 0)

    def mod_map(i):
        lat_row = jnp.maximum(i - n_ctx_tiles, 0) // tiles_per_mod
        return (jnp.where(i < n_ctx_tiles, CTX_MOD_ROW, lat_row), 0, 0)

    in_specs, args, x_specs, x_shapes = [], [], [], []
    for j, (ya, yb, x) in enumerate(streams):
        tmap = tile_maps(len(streams) == 2 and j == 0)
        in_specs += [pl.BlockSpec((tm, half), tmap), pl.BlockSpec((tm, half), tmap), pl.BlockSpec((tm, D_MODEL), tmap)]
        args += [ya, yb, x]
        x_specs.append(pl.BlockSpec((tm, D_MODEL), tmap))
        x_shapes.append(jax.ShapeDtypeStruct(x.shape, F32))
    in_specs += [
        pl.BlockSpec((2 * half, D_MODEL), lambda i: (0, 0)),
        pl.BlockSpec((None, N_MOD, D_MODEL), mod_map),
        pl.BlockSpec((1, D_MODEL), lambda i: (0, 0)),
        pl.BlockSpec((D_MODEL, LANES), lambda i: (0, 0)),
        pl.BlockSpec((1, LANES), lambda i: (0, 0)),
    ]
    args += [w_out, mod, norm_w.reshape(1, D_MODEL), wr, br]
    return pl.pallas_call(
        functools.partial(_outproj_kernel, tm=tm, n_ctx_tiles=n_ctx_tiles),
        grid=(total,),
        in_specs=in_specs,
        out_specs=x_specs + [
            pl.BlockSpec((tm, D_MODEL), lambda i: (i, 0)),
            pl.BlockSpec((tm, LANES), lambda i: (i, 0)),
            pl.BlockSpec((tm, LANES), lambda i: (i, 0)),
            pl.BlockSpec((1, LANES), lambda i: (0, 0)),
        ],
        out_shape=x_shapes + [
            jax.ShapeDtypeStruct((total * tm, D_MODEL), F32),
            jax.ShapeDtypeStruct((total * tm, LANES), I32),
            jax.ShapeDtypeStruct((total * tm, LANES), F32),
            jax.ShapeDtypeStruct((1, LANES), F32),
        ],
        scratch_shapes=[pltpu.VMEM((1, LANES), F32)],
        compiler_params=_cparams(("arbitrary",)),
        name="outproj_router",
    )(*args)


def _moe_kernel(bexp_ref, nval_ref, nused_ref, dst_ref, h_hbm, wg_ref, wu_ref, wd_ref, o_hbm,
                xbuf, ybuf, wgb, wub, wdb, gsem, ssem, *, n_tok):
    i = pl.program_id(0)
    n_used = nused_ref[0]

    def gather_copy(src, slot, r):
        return pltpu.make_async_copy(h_hbm.at[pl.ds(src, 1), :], xbuf.at[slot, pl.ds(r, 1), :], gsem.at[slot])

    def scatter_copy(dst, slot, r):
        return pltpu.make_async_copy(ybuf.at[slot, pl.ds(r, 1), :], o_hbm.at[pl.ds(dst, 1), :], ssem.at[slot])

    def start_gather(blk, slot):
        def body(r, carry):
            dst = dst_ref[blk * MOE_BLOCK + r]
            src = dst - jnp.where(dst >= n_tok, n_tok, 0)
            gather_copy(src, slot, r).start()
            return carry

        lax.fori_loop(0, nval_ref[blk], body, 0)

    def wait_gather(blk, slot):
        def body(r, carry):
            gather_copy(0, slot, 0).wait()
            return carry

        lax.fori_loop(0, nval_ref[blk], body, 0)

    def start_scatter(blk, slot):
        def body(r, carry):
            scatter_copy(dst_ref[blk * MOE_BLOCK + r], slot, r).start()
            return carry

        lax.fori_loop(0, nval_ref[blk], body, 0)

    def wait_scatter(blk, slot):
        def body(r, carry):
            scatter_copy(0, slot, 0).wait()
            return carry

        lax.fori_loop(0, nval_ref[blk], body, 0)

    @pl.when(i < n_used)
    def _():
        slot = i % 2

        @pl.when(i == 0)
        def _():
            xbuf[...] = jnp.zeros_like(xbuf)
            start_gather(0, 0)

        @pl.when(i + 1 < n_used)
        def _():
            start_gather(i + 1, 1 - slot)

        new_expert = jnp.logical_or(i == 0, bexp_ref[i] != bexp_ref[jnp.maximum(i - 1, 0)])

        @pl.when(new_expert)
        def _():
            wgb[...] = wg_ref[...].astype(BF16)
            wub[...] = wu_ref[...].astype(BF16)
            wdb[...] = wd_ref[...].astype(BF16)

        wait_gather(i, slot)
        xb = xbuf[slot].astype(BF16)
        hid = (_silu(_dot(xb, wgb[...])) * _dot(xb, wub[...])).astype(BF16)
        y = _dot(hid, wdb[...])

        @pl.when(i >= 2)
        def _():
            wait_scatter(i - 2, slot)

        ybuf[slot] = y
        start_scatter(i, slot)

        @pl.when(i == n_used - 1)
        def _():
            wait_scatter(i, slot)

            @pl.when(i >= 1)
            def _():
                wait_scatter(i - 1, 1 - slot)


def _moe(h2, ri, cnt, w_gate, w_up, w_down):
    n_tok = h2.shape[0]
    n_assign = 2 * n_tok
    n_blocks = -(-(n_assign + MOE_EXPERTS * (MOE_BLOCK - 1)) // MOE_BLOCK)
    counts = cnt[0, :MOE_EXPERTS].astype(I32)
    padded = (counts + MOE_BLOCK - 1) // MOE_BLOCK * MOE_BLOCK
    p_end = jnp.cumsum(padded)
    p_start = p_end - padded
    e = ri[:, 0:2]
    dest = p_start[e] + ri[:, 2:4]
    dst_val = jnp.arange(n_tok, dtype=I32)[:, None] + jnp.array([0, n_tok], I32)[None, :]
    dst_row = jnp.zeros((n_blocks * MOE_BLOCK,), I32).at[dest.reshape(-1)].set(dst_val.reshape(-1))
    blk0 = jnp.arange(n_blocks, dtype=I32) * MOE_BLOCK
    n_used = (p_end[-1] // MOE_BLOCK).astype(I32)
    bexp = jnp.minimum(jnp.searchsorted(p_end, blk0, side="right"), MOE_EXPERTS - 1).astype(I32)
    used = jnp.arange(n_blocks, dtype=I32) < n_used
    bexp = jnp.where(used, bexp, bexp[jnp.maximum(n_used - 1, 0)])
    nval = jnp.where(used, jnp.clip(counts[bexp] - (blk0 - p_start[bexp]), 0, MOE_BLOCK), 0).astype(I32)

    def wspec(shape):
        return pl.BlockSpec((None,) + shape, lambda i, bexp_ref, *_: (bexp_ref[i], 0, 0))

    grid_spec = pltpu.PrefetchScalarGridSpec(
        num_scalar_prefetch=4,
        grid=(n_blocks,),
        in_specs=[
            pl.BlockSpec(memory_space=pl.ANY),
            wspec((D_MODEL, MOE_HIDDEN)),
            wspec((D_MODEL, MOE_HIDDEN)),
            wspec((MOE_HIDDEN, D_MODEL)),
        ],
        out_specs=pl.BlockSpec(memory_space=pl.ANY),
        scratch_shapes=[
            pltpu.VMEM((2, MOE_BLOCK, D_MODEL), F32),
            pltpu.VMEM((2, MOE_BLOCK, D_MODEL), F32),
            pltpu.VMEM((D_MODEL, MOE_HIDDEN), BF16),
            pltpu.VMEM((D_MODEL, MOE_HIDDEN), BF16),
            pltpu.VMEM((MOE_HIDDEN, D_MODEL), BF16),
            pltpu.SemaphoreType.DMA((2,)),
            pltpu.SemaphoreType.DMA((2,)),
        ],
    )
    return pl.pallas_call(
        functools.partial(_moe_kernel, n_tok=n_tok),
        grid_spec=grid_spec,
        out_shape=jax.ShapeDtypeStruct((n_assign, D_MODEL), F32),
        compiler_params=_cparams(("arbitrary",)),
        name="moe_experts",
    )(bexp, nval, n_used.reshape(1), dst_row, h2, w_gate, w_up, w_down)


def _combine_kernel(x_ref, o0_ref, o1_ref, rg_ref, mod_ref, fw_ref, out_ref, *, final):
    rg = rg_ref[...]
    moe = rg[:, 0:1] * o0_ref[...] + rg[:, 1:2] * o1_ref[...]
    xn = x_ref[...] + mod_ref[5:6, :] * moe
    if final:
        xn = xn * lax.rsqrt(jnp.mean(xn * xn, -1, keepdims=True) + EPS) * fw_ref[...]
    out_ref[...] = xn


def _combine(x, moe_out, rg, mod, fw, *, tm, rows_per_mod, row_off, n_tok, final):
    rows = x.shape[0]
    if rows_per_mod is None:
        mod_map = lambda i: (CTX_MOD_ROW, 0, 0)
    else:
        tiles_per_mod = rows_per_mod // tm
        mod_map = lambda i: (i // tiles_per_mod, 0, 0)
    off0 = row_off // tm
    off1 = (n_tok + row_off) // tm
    return pl.pallas_call(
        functools.partial(_combine_kernel, final=final),
        grid=(rows // tm,),
        in_specs=[
            pl.BlockSpec((tm, D_MODEL), lambda i: (i, 0)),
            pl.BlockSpec((tm, D_MODEL), lambda i: (i + off0, 0)),
            pl.BlockSpec((tm, D_MODEL), lambda i: (i + off1, 0)),
            pl.BlockSpec((tm, LANES), lambda i: (i + off0, 0)),
            pl.BlockSpec((None, N_MOD, D_MODEL), mod_map),
            pl.BlockSpec((1, D_MODEL), lambda i: (0, 0)),
        ],
        out_specs=pl.BlockSpec((tm, D_MODEL), lambda i: (i, 0)),
        out_shape=jax.ShapeDtypeStruct((rows, D_MODEL), F32),
        compiler_params=_cparams(("arbitrary",)),
        name="moe_combine",
    )(x, moe_out, moe_out, rg, mod, fw.reshape(1, D_MODEL))


def _dn_kernel(q_ref, k_ref, v_ref, z_ref, kc_ref, vc_ref, abl_ref, abc_ref, cwq_ref, cwk_ref, cwv_ref,
               alog_ref, dtb_ref, nw_ref, y_ref,
               xf_scr, qn_scr, kn_scr, vn_scr, kcn_scr, vcn_scr, s_scr, o_scr):
    c = DN_CHUNK
    h = pl.program_id(1)
    rb = 256

    def conv_prep(src_ref, cw_ref, dst_ref, t, mode):
        xf_scr[0:8, :] = jnp.zeros((8, HEAD_DIM), F32)
        xf_scr[8 + t:16 + t, :] = jnp.zeros((8, HEAD_DIM), F32)
        for cc in range(t // rb):
            xf_scr[8 + cc * rb:8 + (cc + 1) * rb, :] = src_ref[cc * rb:(cc + 1) * rb, :].astype(F32)
        row = lax.broadcasted_iota(I32, (rb, HEAD_DIM), 0)
        for cc in range(t // rb):
            base = 8 + cc * rb
            cur = xf_scr[base:base + rb, :]
            xp = jnp.where(row == 0, xf_scr[base - 1:base, :], pltpu.roll(cur, 1, 0))
            xn = jnp.where(row == rb - 1, xf_scr[base + rb:base + rb + 1, :], pltpu.roll(cur, rb - 1, 0))
            y = _silu(cw_ref[0:1, :] * xp + cw_ref[1:2, :] * cur + cw_ref[2:3, :] * xn)
            if mode != "v":
                y = y * lax.rsqrt(jnp.sum(y * y, -1, keepdims=True) + EPS)
            if mode == "q":
                y = y * (HEAD_DIM ** -0.5)
            dst_ref[cc * rb:(cc + 1) * rb, :] = y.astype(BF16)

    conv_prep(q_ref, cwq_ref, qn_scr, SEQ, "q")
    conv_prep(k_ref, cwk_ref, kn_scr, SEQ, "k")
    conv_prep(v_ref, cwv_ref, vn_scr, SEQ, "v")
    conv_prep(kc_ref, cwk_ref, kcn_scr, CTX_LEN, "k")
    conv_prep(vc_ref, cwv_ref, vcn_scr, CTX_LEN, "v")

    s_scr[...] = jnp.zeros_like(s_scr)
    o_scr[...] = jnp.zeros_like(o_scr)

    ii = lax.broadcasted_iota(I32, (c, c), 0)
    jj = lax.broadcasted_iota(I32, (c, c), 1)
    eye = jnp.where(ii == jj, 1.0, 0.0)
    ones = jnp.ones((c, c), F32)
    bd16 = (ii // 16) == (jj // 16)
    bd32 = (ii // 32) == (jj // 32)
    lane = lax.broadcasted_iota(I32, (c, LANES), 1)

    def step(d, k_r, v_r, ab_r, n, q_r=None):
        r = pl.ds(pl.multiple_of(n * c, c), c)
        k, v = k_r[r, :], v_r[r, :]
        ab = ab_r[r, :]
        a_col = jnp.sum(jnp.where(lane == d * HEADS + h, ab, 0.0), -1, keepdims=True)
        b_col = jnp.sum(jnp.where(lane == (2 + d) * HEADS + h, ab, 0.0), -1, keepdims=True)
        xx = a_col + dtb_ref[d:d + 1, :]
        softplus = jnp.maximum(xx, 0.0) + jnp.log(1.0 + jnp.exp(-jnp.abs(xx)))
        glog_b = jnp.broadcast_to(-jnp.exp(alog_ref[d:d + 1, :]) * softplus, (c, LANES))
        beta_b = jnp.broadcast_to(jax.nn.sigmoid(b_col), (c, LANES))
        incl = (ii >= jj) if d == 0 else (ii <= jj)
        strict = (ii > jj) if d == 0 else (ii < jj)
        g_b = _dot(jnp.where(incl, 1.0, 0.0), glog_b, HIGHEST)
        g_sq = g_b[:, :c]
        g_row = _dot(ones, eye * g_sq, HIGHEST)
        decay = jnp.where(incl, jnp.exp(jnp.where(incl, g_sq - g_row, 0.0)), 0.0)
        nmat = -jnp.where(strict, beta_b[:, :c] * _dot_nt(k, k) * decay, 0.0)
        n1 = jnp.where(bd16, nmat, 0.0)
        n2 = _dot(n1, n1, HIGHEST)
        n4 = _dot(n2, n2, HIGHEST)
        n8 = _dot(n4, n4, HIGHEST)
        t = eye + n1
        t = t + _dot(t, n2, HIGHEST)
        t = t + _dot(t, n4, HIGHEST)
        t = t + _dot(t, n8, HIGHEST)
        off32 = jnp.where(bd32 & jnp.logical_not(bd16), nmat, 0.0)
        t = t + _dot(t, _dot(off32, t, HIGHEST), HIGHEST)
        off64 = jnp.where(bd32, 0.0, nmat)
        t = t + _dot(t, _dot(off64, t, HIGHEST), HIGHEST)
        eg = jnp.exp(g_b)
        kf = k.astype(F32)
        rhs = jnp.concatenate([v.astype(F32) * beta_b, kf * (beta_b * eg)], -1).astype(BF16)
        sol = _dot(t.astype(BF16), rhs)
        s = s_scr[d]
        sb = s.astype(BF16)
        v_new = (sol[:, :HEAD_DIM] - _dot(sol[:, HEAD_DIM:].astype(BF16), sb)).astype(BF16)
        if q_r is not None:
            q = q_r[r, :]
            attn = (_dot_nt(q, k) * decay).astype(BF16)
            inter = _dot((q.astype(F32) * eg).astype(BF16), sb)
            o_scr[r, :] += inter + _dot(attn, v_new)
        g_last = g_b[c - 1:c, :] if d == 0 else g_b[0:1, :]
        k_tail = (kf * jnp.exp(g_last - g_b)).astype(BF16)
        s_scr[d] = s * jnp.exp(g_last) + _dot_tn(k_tail, v_new)

    n_c = CTX_LEN // c
    n_l = SEQ // c
    for n in range(n_c):
        step(0, kcn_scr, vcn_scr, abc_ref, n)
        step(1, kcn_scr, vcn_scr, abc_ref, n_c - 1 - n)

    def body(n, carry):
        step(0, kn_scr, vn_scr, abl_ref, n, qn_scr)
        step(1, kn_scr, vn_scr, abl_ref, n_l - 1 - n, qn_scr)
        return carry

    lax.fori_loop(0, n_l, body, 0)

    def fbody(tt, carry):
        r = pl.ds(pl.multiple_of(tt * rb, rb), rb)
        o = o_scr[r, :]
        y = o * lax.rsqrt(jnp.mean(o * o, -1, keepdims=True) + EPS) * nw_ref[...]
        y_ref[r, :] = (y * _silu(z_ref[r, :].astype(F32))).astype(y_ref.dtype)
        return carry

    lax.fori_loop(0, SEQ // rb, fbody, 0)


def _deltanet(main_l, kv_c, ab_l, ab_c, conv_w, a_log, dt_bias, norm_w):
    cw = conv_w.reshape(3, 3 * HEADS, HEAD_DIM).transpose(1, 0, 2)
    al = a_log.T.reshape(HEADS, 2, 1)
    dtb = dt_bias.T.reshape(HEADS, 2, 1)

    def cols(rows, off):
        return pl.BlockSpec((rows, HEAD_DIM), lambda b, h: (b, off + h))

    def cwspec(off):
        return pl.BlockSpec((None, 3, HEAD_DIM), lambda b, h: (off + h, 0, 0))

    in_specs = ([cols(SEQ, j * HEADS) for j in range(4)] + [cols(CTX_LEN, 0), cols(CTX_LEN, HEADS)]
                + [pl.BlockSpec((SEQ, LANES), lambda b, h: (b, 0)), pl.BlockSpec((CTX_LEN, LANES), lambda b, h: (b, 0))]
                + [cwspec(0), cwspec(HEADS), cwspec(2 * HEADS)]
                + [pl.BlockSpec((None, 2, 1), lambda b, h: (h, 0, 0)), pl.BlockSpec((None, 2, 1), lambda b, h: (h, 0, 0)),
                   pl.BlockSpec((1, HEAD_DIM), lambda b, h: (0, 0))])
    return pl.pallas_call(
        _dn_kernel,
        grid=(BATCH, HEADS),
        in_specs=in_specs,
        out_specs=cols(SEQ, 0),
        out_shape=jax.ShapeDtypeStruct((N_LAT, HEADS * HEAD_DIM), BF16),
        scratch_shapes=[
            pltpu.VMEM((SEQ + 16, HEAD_DIM), F32),
            pltpu.VMEM((SEQ, HEAD_DIM), BF16),
            pltpu.VMEM((SEQ, HEAD_DIM), BF16),
            pltpu.VMEM((SEQ, HEAD_DIM), BF16),
            pltpu.VMEM((CTX_LEN, HEAD_DIM), BF16),
            pltpu.VMEM((CTX_LEN, HEAD_DIM), BF16),
            pltpu.VMEM((2, HEAD_DIM, HEAD_DIM), F32),
            pltpu.VMEM((SEQ, HEAD_DIM), F32),
        ],
        compiler_params=_cparams(("arbitrary", "arbitrary")),
        name="deltanet",
    )(main_l, main_l, main_l, main_l, kv_c, kv_c, ab_l, ab_c, cw, cw, cw, al, dtb, norm_w.reshape(1, HEAD_DIM))


def _sgu_kernel(u_ref, s_ref, w_ref, b_ref, o_ref, *, n_chunks):
    for ch in range(n_chunks):
        rows = slice(ch * SGU_CHUNK, (ch + 1) * SGU_CHUNK)
        for g in range(SGU_GROUPS):
            sl = slice(g * SGU_CH, (g + 1) * SGU_CH)
            v = _gelu(s_ref[rows, sl].astype(F32))
            mu = jnp.mean(v, -1, keepdims=True)
            vc = v - mu
            var = jnp.mean(vc * vc, -1, keepdims=True)
            vn = (vc * lax.rsqrt(var + EPS)).astype(BF16)
            mixed = _dot(w_ref[g].astype(BF16), vn) + b_ref[:, g:g + 1]
            o_ref[rows, sl] = (_gelu(u_ref[rows, sl].astype(F32)) * mixed).astype(o_ref.dtype)


def _sgu(main_l, sgu_w, sgu_b):
    n_chunks = 4
    tm = n_chunks * SGU_CHUNK
    width = SGU_GROUPS * SGU_CH
    return pl.pallas_call(
        functools.partial(_sgu_kernel, n_chunks=n_chunks),
        grid=(N_LAT // tm,),
        in_specs=[
            pl.BlockSpec((tm, width), lambda i: (i, 4)),
            pl.BlockSpec((tm, width), lambda i: (i, 5)),
            pl.BlockSpec((SGU_GROUPS, SGU_CHUNK, SGU_CHUNK), lambda i: (0, 0, 0)),
            pl.BlockSpec((SGU_CHUNK, SGU_GROUPS), lambda i: (0, 0)),
        ],
        out_specs=pl.BlockSpec((tm, width), lambda i: (i, 0)),
        out_shape=jax.ShapeDtypeStruct((N_LAT, width), BF16),
        compiler_params=_cparams(("arbitrary",)),
        name="spatial_gating",
    )(main_l, main_l, sgu_w, sgu_b.T)


def _router_weights(wg, bg, we, be):
    pad = LANES - MOE_GROUPS - MOE_EXPERTS
    wr = jnp.concatenate([wg, we, jnp.zeros((D_MODEL, pad), F32)], 1)
    br = jnp.concatenate([bg, be, jnp.zeros((pad,), F32)])[None, :]
    return wr, br


def kernel(x, c, ctx, c_ctx, norm1_w, norm2_w, ada_w, ada_b, even_w_in, ret_decay_logit, ret_gn_w, even_w_out,
           odd_w_in, dn_conv_w, dn_a_log, dn_dt_bias, dn_norm_w, sgu_w, sgu_b, odd_w_out, router_g_w, router_g_b,
           router_e_w, router_e_b, moe_w_gate, moe_w_up, moe_w_down, final_norm_w):
    mod = _modulation(c, c_ctx, ada_w, ada_b)
    x_lat = x.reshape(N_LAT, D_MODEL)
    x_ctx = ctx.reshape(N_CTX, D_MODEL)
    tm_l, tm_c, tm_r = 512, 256, 256
    n_tok0 = N_CTX + N_LAT

    hw = HEADS * HEAD_DIM
    w_in0 = even_w_in[0].astype(BF16)
    segs0 = ((0, hw, "q", 0, 0), (hw, hw, "k", 0, hw), (2 * hw, hw, "v", 0, 2 * hw), (3 * hw, hw, "g", 0, 3 * hw),
             (4 * hw, hw, "f", 0, 4 * hw))
    outs0 = ((5 * hw, BF16),)
    q_scale = HEAD_DIM ** -0.5
    (lat_l,) = _inproj(x_lat, norm1_w[0], mod[0], w_in0, segs0, outs0, tm=tm_l, rows_per_mod=SEQ,
                       rope_tabs=_rope_tables(), q_scale=q_scale)
    (lat_c,) = _inproj(x_ctx, norm1_w[0], mod[0], w_in0, segs0, outs0, tm=tm_c, rows_per_mod=None, q_scale=q_scale)
    ret_l, ret_c = _retention(lat_l, lat_c, ret_decay_logit[0], ret_gn_w[0])
    fou_l = _fourier(lat_l, SEQ, 512)
    fou_c = _fourier(lat_c, CTX_LEN, CTX_LEN)
    wr, br = _router_weights(router_g_w[0], router_g_b[0], router_e_w[0], router_e_b[0])
    w_out0 = even_w_out[0].astype(BF16)
    x_ctx, x_lat, h2, ri, rg, cnt = _outproj([(ret_c, fou_c, x_ctx), (ret_l, fou_l, x_lat)], w_out0, mod[0],
                                             norm2_w[0], wr, br, tm=tm_r)
    moe_out = _moe(h2, ri, cnt, moe_w_gate[0], moe_w_up[0], moe_w_down[0])
    x_ctx = _combine(x_ctx, moe_out, rg, mod[0], final_norm_w, tm=tm_r, rows_per_mod=None, row_off=0,
                     n_tok=n_tok0, final=False)
    x_lat = _combine(x_lat, moe_out, rg, mod[0], final_norm_w, tm=tm_r, rows_per_mod=SEQ, row_off=N_CTX,
                     n_tok=n_tok0, final=False)

    w1 = odd_w_in[0]
    o_z, o_a = 4 * hw, 4 * hw + 4 * HEADS
    o_u = o_a
    ab_pad = jnp.zeros((D_MODEL, LANES - 4 * HEADS), F32)
    w_ab = jnp.concatenate([w1[:, o_z:o_a], ab_pad], 1)
    w_in1_l = jnp.concatenate([w1[:, :o_z], w1[:, o_u:], w_ab], 1).astype(BF16)
    w_in1_c = jnp.concatenate([w1[:, hw:3 * hw], w_ab], 1).astype(BF16)
    segs1_l = tuple((j * hw, hw, "p", 0, j * hw) for j in range(6)) + ((6 * hw, LANES, "p", 1, 0),)
    segs1_c = tuple((j * hw, hw, "p", 0, j * hw) for j in range(2)) + ((2 * hw, LANES, "p", 1, 0),)
    main_l, ab_l = _inproj(x_lat, norm1_w[1], mod[1], w_in1_l, segs1_l, ((6 * hw, BF16), (LANES, F32)),
                           tm=tm_l, rows_per_mod=SEQ)
    kv_c, ab_c = _inproj(x_ctx, norm1_w[1], mod[1], w_in1_c, segs1_c, ((2 * hw, BF16), (LANES, F32)),
                         tm=tm_c, rows_per_mod=None)
    dn_l = _deltanet(main_l, kv_c, ab_l, ab_c, dn_conv_w[0], dn_a_log[0], dn_dt_bias[0], dn_norm_w[0])
    sg_l = _sgu(main_l, sgu_w[0], sgu_b[0])
    wr, br = _router_weights(router_g_w[1], router_g_b[1], router_e_w[1], router_e_b[1])
    x_lat, h2, ri, rg, cnt = _outproj([(dn_l, sg_l, x_lat)], odd_w_out[0].astype(BF16), mod[1], norm2_w[1], wr, br,
                                      tm=tm_r)
    moe_out = _moe(h2, ri, cnt, moe_w_gate[1], moe_w_up[1], moe_w_down[1])
    out = _combine(x_lat, moe_out, rg, mod[1], final_norm_w, tm=tm_r, rows_per_mod=SEQ, row_off=0,
                   n_tok=N_LAT, final=True)
    return out.reshape(BATCH, SEQ, D_MODEL)
```
